```python
import math
import jax
import jax.numpy as jnp
from jax import lax
import numpy as np

D_MODEL = 4096
BATCH = 1
SEQ = 16384
DEPTH = 4

GRID_W = 64
CTX_LEN = 256
EPS = 1e-6
HEAD_DIM = 128

SSD_WIDTH = D_MODEL // 2
SSD_HEAD_DIM = 64
SSD_HEADS = SSD_WIDTH // SSD_HEAD_DIM
SSD_GROUPS = 4
SSD_STATE = 128
SSD_CHUNK = 128
SSD_CONV = 3
SSD_XB = SSD_WIDTH + SSD_GROUPS * SSD_STATE
SSD_CONV_DIM = SSD_XB + SSD_GROUPS * SSD_STATE

NA_WIDTH = D_MODEL // 4
NA_HEADS = NA_WIDTH // HEAD_DIM
NA_WIN_ROWS = 8
NA_WIN_COLS = 16

GQA_WIDTH = D_MODEL // 4
GQA_HEADS = GQA_WIDTH // HEAD_DIM
GQA_KV_HEADS = GQA_HEADS // 4
GQA_BLOCK = 128
ROPE_THETA = 10000.0

MIX_WIDTH = SSD_WIDTH + NA_WIDTH + GQA_WIDTH

IN_SIZES_KV = (SSD_XB, 2 * SSD_HEADS, NA_WIDTH, NA_WIDTH, GQA_KV_HEADS * HEAD_DIM, GQA_KV_HEADS * HEAD_DIM)
IN_SIZES_Q = (SSD_GROUPS * SSD_STATE, SSD_WIDTH, NA_WIDTH, NA_WIDTH, GQA_WIDTH, GQA_WIDTH)
KV_COLS = sum(IN_SIZES_KV)
IN_COLS = KV_COLS + sum(IN_SIZES_Q)

kernel_name = 'hybrid_ssd_natten_gqa_prefix_dit'


def split_cols(p, sizes):
    idx = np.cumsum(sizes)[:-1].tolist()
    return jnp.split(p, idx, axis=-1)


def split_heads(t, n):
    return t.reshape(t.shape[0], t.shape[1], n, HEAD_DIM)


def flip(t):
    return jnp.flip(t, axis=1)


def rmsnorm(x, g):
    xf = x.astype(jnp.float32)
    y = xf * lax.rsqrt(jnp.mean(xf * xf, axis=-1, keepdims=True) + EPS)
    return (y * g.astype(jnp.float32)).astype(x.dtype)


def dwconv(u, w, b):
    out = lax.conv_general_dilated(
        u, w[:, None, :].astype(u.dtype), window_strides=(1,),
        padding=[(SSD_CONV // 2, SSD_CONV // 2)],
        dimension_numbers=('NWC', 'WIO', 'NWC'), feature_group_count=u.shape[-1])
    return out + b.astype(u.dtype)


def segsum(a):
    cs = jnp.cumsum(a, axis=-1)
    diff = cs[..., :, None] - cs[..., None, :]
    l = a.shape[-1]
    mask = jnp.tril(jnp.ones((l, l), dtype=bool))
    return jnp.where(mask, diff, -jnp.inf)


def ssd_prepare(xbc_raw, dt_raw, conv_w, conv_b, dt_bias):
    n = xbc_raw.shape[-1]
    u = jax.nn.silu(dwconv(xbc_raw, conv_w[:, :n], conv_b[:n])).astype(jnp.float32)
    b, L, _ = u.shape
    gn = (b, L, SSD_GROUPS, SSD_STATE)
    xs = u[..., :SSD_WIDTH].reshape(b, L, SSD_HEADS, SSD_HEAD_DIM)
    bm = u[..., SSD_WIDTH:SSD_XB].reshape(gn)
    cm = u[..., SSD_XB:].reshape(gn) if n > SSD_XB else None
    dt = jax.nn.softplus(dt_raw.astype(jnp.float32).reshape(b, L, 2, SSD_HEADS) + dt_bias.astype(jnp.float32))
    return xs, bm, cm, dt


def ssd_chunked(x, dt, A, B, C, init):
    b, L, H, P = x.shape
    G, N = B.shape[-2:]
    R = H // G
    nc = L // SSD_CHUNK
    xr = x.reshape(b, nc, SSD_CHUNK, G, R, P)
    dtr = dt.reshape(b, nc, SSD_CHUNK, G, R)
    br = B.reshape(b, nc, SSD_CHUNK, G, N)
    cr = C.reshape(b, nc, SSD_CHUNK, G, N)
    a = (dtr * A.reshape(G, R)).transpose(0, 3, 4, 1, 2)
    a_cum = jnp.cumsum(a, axis=-1)
    xd = xr * dtr[..., None]
    cb = jnp.einsum('bclgn,bcsgn->bgcls', cr, br)
    y_diag = jnp.einsum('bgrcls,bcsgrp->bclgrp', cb[:, :, None] * jnp.exp(segsum(a)), xd)
    decay_states = jnp.exp(a_cum[..., -1:] - a_cum)
    chunk_states = jnp.einsum('bclgn,bgrcl,bclgrp->bcgrpn', br, decay_states, xd)
    chunk_decay = jnp.exp(a_cum[..., -1])

    def step(s, inp):
        st, dec = inp
        return dec[..., None, None] * s + st, s

    final, prev = lax.scan(step, init, (jnp.moveaxis(chunk_states, 1, 0), jnp.moveaxis(chunk_decay, -1, 0)))
    prev = jnp.moveaxis(prev, 0, 1)
    y_off = jnp.einsum('bclgn,bcgrpn,bgrcl->bclgrp', cr, prev, jnp.exp(a_cum))
    return (y_diag + y_off).reshape(b, L, H, P), final


def ssd_final_state(x, dt, A, B):
    b, L, H, P = x.shape
    G = B.shape[-2]
    R = H // G
    dtr = dt.reshape(b, L, G, R)
    a_cum = jnp.cumsum(dtr * A.reshape(G, R), axis=1)
    w = jnp.exp(a_cum[:, -1:] - a_cum) * dtr
    return jnp.einsum('blgr,blgn,blgrp->bgrpn', w, B, x.reshape(b, L, G, R, P))


def ssd_output(y, xs, d_skip, z, g):
    b, L = y.shape[:2]
    y = (y + xs * d_skip.astype(jnp.float32)[:, None]).reshape(b, L, SSD_WIDTH)
    u = (y * jax.nn.silu(z.astype(jnp.float32))).reshape(b, L, SSD_GROUPS, SSD_WIDTH // SSD_GROUPS)
    u = u * lax.rsqrt(jnp.mean(u * u, axis=-1, keepdims=True) + EPS)
    return (u.reshape(b, L, SSD_WIDTH) * g.astype(jnp.float32)).astype(z.dtype)


def dense_attention(q, k, v):
    b, tq, hq, dh = q.shape
    hkv = k.shape[2]
    qg = q.reshape(b, tq, hkv, hq // hkv, dh)
    s = jnp.einsum('bqgrd,bkgd->bgrqk', qg, k).astype(jnp.float32) * (dh ** -0.5)
    p = jax.nn.softmax(s, axis=-1).astype(v.dtype)
    o = jnp.einsum('bgrqk,bkgd->bqgrd', p, v)
    return o.reshape(b, tq, hq * dh)


def blocked_attention(q, k_all, v_all):
    b, S, hq, dh = q.shape
    nb = S // GQA_BLOCK
    qb = jnp.moveaxis(q.reshape(b, nb, GQA_BLOCK, hq, dh), 1, 0)
    o = lax.map(lambda qq: dense_attention(qq, k_all, v_all), qb)
    return jnp.moveaxis(o, 0, 1).reshape(b, S, hq * dh)


def axial_rope(S):
    t = jnp.arange(S)
    row = (t // GRID_W).astype(jnp.float32)
    col = (t % GRID_W).astype(jnp.float32)
    n = HEAD_DIM // 4
    freqs = ROPE_THETA ** (-jnp.arange(n, dtype=jnp.float32) / n)
    ang = jnp.concatenate([row[:, None] * freqs, col[:, None] * freqs], axis=-1)
    return jnp.cos(ang), jnp.sin(ang)


def apply_rope(x, cos, sin):
    xf = x.astype(jnp.float32).reshape(*x.shape[:-1], HEAD_DIM // 2, 2)
    x0, x1 = xf[..., 0], xf[..., 1]
    cc = cos[None, :, None, :]
    ss = sin[None, :, None, :]
    out = jnp.stack([x0 * cc - x1 * ss, x0 * ss + x1 * cc], axis=-1).reshape(x.shape)
    return out.astype(x.dtype)


def neighbourhood_attention(q, k, v, kc, vc, rpb, rows):
    b, S, H, dh = q.shape
    wr = min(NA_WIN_ROWS, rows)
    r = jnp.arange(rows)
    col = jnp.arange(GRID_W)
    row_idx = jnp.clip(r - wr // 2, 0, rows - wr)[:, None] + jnp.arange(wr)[None, :]
    col_start = jnp.clip(col - NA_WIN_COLS // 2, 0, GRID_W - NA_WIN_COLS)
    col_mask = (col[None, :] >= col_start[:, None]) & (col[None, :] < col_start[:, None] + NA_WIN_COLS)
    dr = row_idx - r[:, None] + (NA_WIN_ROWS - 1)
    dc = jnp.clip(col[None, :] - col[:, None], -(NA_WIN_COLS - 1), NA_WIN_COLS - 1) + (NA_WIN_COLS - 1)
    bias = rpb[:, dr[:, None, :, None], dc[None, :, None, :]].astype(jnp.float32)
    bias = jnp.where(col_mask[None, None, :, None, :], bias, -jnp.inf)
    qg = q.reshape(b, rows, GRID_W, H, dh)
    kw = jnp.take(k.reshape(b, rows, GRID_W, H, dh), row_idx, axis=1)
    vw = jnp.take(v.reshape(b, rows, GRID_W, H, dh), row_idx, axis=1)
    scale = dh ** -0.5
    s_win = jnp.einsum('bixhd,bijyhd->bhixjy', qg, kw).astype(jnp.float32) * scale + bias[None]
    s_ctx = jnp.einsum('bixhd,bthd->bhixt', qg, kc).astype(jnp.float32) * scale
    nwin = wr * GRID_W
    s = jnp.concatenate([s_win.reshape(b, H, rows, GRID_W, nwin), s_ctx], axis=-1)
    p = jax.nn.softmax(s, axis=-1).astype(v.dtype)
    p_win = p[..., :nwin].reshape(b, H, rows, GRID_W, wr, GRID_W)
    p_ctx = p[..., nwin:]
    o = jnp.einsum('bhixjy,bijyhd->bixhd', p_win, vw) + jnp.einsum('bhixt,bthd->bixhd', p_ctx, vc)
    return o.reshape(b, S, H * dh)


def setup_inputs(seed: int = 0) -> dict:
    key = jax.random.key(seed)
    ks = jax.random.split(key, 20)

    def nrm(k, shape, scale):
        return jax.random.normal(k, shape, jnp.float32) * scale

    dt0 = jnp.exp(jax.random.uniform(ks[11], (DEPTH, 2, SSD_HEADS), jnp.float32,
                                     minval=math.log(1e-3), maxval=math.log(1e-1)))
    return {
        'x': nrm(ks[0], (BATCH, SEQ, D_MODEL), 1.0),
        'c': nrm(ks[1], (BATCH, D_MODEL), 1.0),
        'ctx': nrm(ks[2], (BATCH, CTX_LEN, D_MODEL), 1.0),
        'c_ctx': nrm(ks[3], (D_MODEL,), 1.0),
        'w_mod': nrm(ks[4], (DEPTH, D_MODEL, 3 * D_MODEL), D_MODEL ** -0.5),
        'b_mod': nrm(ks[5], (DEPTH, 3 * D_MODEL), 0.02),
        'norm_g': 1.0 + nrm(ks[6], (DEPTH, D_MODEL), 0.05),
        'w_in': nrm(ks[7], (DEPTH, D_MODEL, IN_COLS), D_MODEL ** -0.5),
        'conv_w': nrm(ks[8], (DEPTH, SSD_CONV, SSD_CONV_DIM), SSD_CONV ** -0.5),
        'conv_b': nrm(ks[9], (DEPTH, SSD_CONV_DIM), 0.02),
        'a_log': jnp.log(jax.random.uniform(ks[10], (DEPTH, 2, SSD_HEADS), jnp.float32, minval=1.0, maxval=16.0)),
        'dt_bias': dt0 + jnp.log(-jnp.expm1(-dt0)),
        'd_skip': 1.0 + nrm(ks[12], (DEPTH, SSD_HEADS), 0.1),
        'ssd_norm_g': 1.0 + nrm(ks[13], (DEPTH, SSD_WIDTH), 0.05),
        'na_rpb': nrm(ks[14], (DEPTH, NA_HEADS, 2 * NA_WIN_ROWS - 1, 2 * NA_WIN_COLS - 1), 0.1),
        'q_norm_g': 1.0 + nrm(ks[15], (DEPTH, HEAD_DIM), 0.05),
        'k_norm_g': 1.0 + nrm(ks[16], (DEPTH, HEAD_DIM), 0.05),
        'w_out': nrm(ks[17], (DEPTH, MIX_WIDTH, D_MODEL), MIX_WIDTH ** -0.5),
        'final_norm_g': 1.0 + nrm(ks[18], (D_MODEL,), 0.05),
    }


def reference(x, c, ctx, c_ctx, w_mod, b_mod, norm_g, w_in, conv_w, conv_b, a_log, dt_bias, d_skip,
              ssd_norm_g, na_rpb, q_norm_g, k_norm_g, w_out, final_norm_g):
    b, S, _ = x.shape
    rows = S // GRID_W
    cos, sin = axial_rope(S)
    silu_c = jax.nn.silu(c)
    silu_cc = jax.nn.silu(c_ctx)
    h, hc = x, ctx
    zero_state = jnp.zeros((ctx.shape[0], SSD_GROUPS, SSD_HEADS // SSD_GROUPS, SSD_HEAD_DIM, SSD_STATE), jnp.float32)
    for l in range(DEPTH):
        last = l == DEPTH - 1
        A = -jnp.exp(a_log[l].astype(jnp.float32))

        shift, scale, gate = jnp.split(silu_c @ w_mod[l] + b_mod[l], 3, axis=-1)
        n_mod_c = 2 if last else 3
        mod_c = jnp.split(silu_cc @ w_mod[l, :, :n_mod_c * D_MODEL] + b_mod[l, :n_mod_c * D_MODEL], n_mod_c)
        xn = rmsnorm(h, norm_g[l]) * (1.0 + scale[:, None]) + shift[:, None]
        xcn = rmsnorm(hc, norm_g[l]) * (1.0 + mod_c[1]) + mod_c[0]

        (xb, dt_raw, na_k, na_v, ga_k, ga_v,
         ssd_c, ssd_z, na_q, na_gate, ga_q, ga_gate) = split_cols(xn @ w_in[l], IN_SIZES_KV + IN_SIZES_Q)
        if last:
            (xb_c, dt_raw_c, na_k_c, na_v_c, ga_k_c, ga_v_c) = split_cols(xcn @ w_in[l, :, :KV_COLS], IN_SIZES_KV)
        else:
            (xb_c, dt_raw_c, na_k_c, na_v_c, ga_k_c, ga_v_c,
             ssd_c_c, ssd_z_c, na_q_c, na_gate_c, ga_q_c, ga_gate_c) = split_cols(xcn @ w_in[l], IN_SIZES_KV + IN_SIZES_Q)

        xs, bm, cm, dt = ssd_prepare(jnp.concatenate([xb, ssd_c], axis=-1), dt_raw, conv_w[l], conv_b[l], dt_bias[l])
        if last:
            xs_c, bm_c, _, dt_c = ssd_prepare(xb_c, dt_raw_c, conv_w[l], conv_b[l], dt_bias[l])
            st_f = ssd_final_state(xs_c, dt_c[:, :, 0], A[0], bm_c)
            st_b = ssd_final_state(flip(xs_c), flip(dt_c[:, :, 1]), A[1], flip(bm_c))
        else:
            xs_c, bm_c, cm_c, dt_c = ssd_prepare(jnp.concatenate([xb_c, ssd_c_c], axis=-1), dt_raw_c,
                                                 conv_w[l], conv_b[l], dt_bias[l])
            yf_c, st_f = ssd_chunked(xs_c, dt_c[:, :, 0], A[0], bm_c, cm_c, zero_state)
            yb_c, st_b = ssd_chunked(flip(xs_c), flip(dt_c[:, :, 1]), A[1], flip(bm_c), flip(cm_c), zero_state)
            ssd_o_c = ssd_output(yf_c + flip(yb_c), xs_c, d_skip[l], ssd_z_c, ssd_norm_g[l])
        yf, _ = ssd_chunked(xs, dt[:, :, 0], A[0], bm, cm, st_f)
        yb, _ = ssd_chunked(flip(xs), flip(dt[:, :, 1]), A[1], flip(bm), flip(cm), st_b)
        ssd_o = ssd_output(yf + flip(yb), xs, d_skip[l], ssd_z, ssd_norm_g[l])

        na_kc = split_heads(na_k_c, NA_HEADS)
        na_vc = split_heads(na_v_c, NA_HEADS)
        na_o = neighbourhood_attention(split_heads(na_q, NA_HEADS), split_heads(na_k, NA_HEADS),
                                       split_heads(na_v, NA_HEADS), na_kc, na_vc, na_rpb[l], rows)
        na_o = na_o * jax.nn.silu(na_gate)

        ga_kc = rmsnorm(split_heads(ga_k_c, GQA_KV_HEADS), k_norm_g[l])
        ga_vc = split_heads(ga_v_c, GQA_KV_HEADS)
        q = apply_rope(rmsnorm(split_heads(ga_q, GQA_HEADS), q_norm_g[l]), cos, sin)
        k = apply_rope(rmsnorm(split_heads(ga_k, GQA_KV_HEADS), k_norm_g[l]), cos, sin)
        k_all = jnp.concatenate([ga_kc, k], axis=1)
        v_all = jnp.concatenate([ga_vc, split_heads(ga_v, GQA_KV_HEADS)], axis=1)
        ga_o = blocked_attention(q, k_all, v_all) * jax.nn.silu(ga_gate)

        out = jnp.concatenate([ssd_o, na_o, ga_o], axis=-1) @ w_out[l]
        h = h + gate[:, None] * out

        if not last:
            na_o_c = dense_attention(split_heads(na_q_c, NA_HEADS), na_kc, na_vc) * jax.nn.silu(na_gate_c)
            ga_qc = rmsnorm(split_heads(ga_q_c, GQA_HEADS), q_norm_g[l])
            ga_o_c = dense_attention(ga_qc, ga_kc, ga_vc) * jax.nn.silu(ga_gate_c)
            out_c = jnp.concatenate([ssd_o_c, na_o_c, ga_o_c], axis=-1) @ w_out[l]
            hc = hc + mod_c[2] * out_c

    return rmsnorm(h, final_norm_g)
```

```python
import functools
import math

import numpy as np
import jax
import jax.numpy as jnp
from jax import lax
from jax.experimental import pallas as pl
from jax.experimental.pallas import tpu as pltpu

F32 = jnp.float32
BF16 = jnp.bfloat16

GRID_W = 64
EPS = 1e-6
HEAD_DIM = 128
SSD_HEAD_DIM = 64
SSD_GROUPS = 4
SSD_STATE = 128
SSD_CONV = 3
NA_WIN_ROWS = 8
NA_WIN_COLS = 16
GQA_REP = 4
ROPE_THETA = 10000.0

LANES = 128
SUBLANES = 8
V7X_VMEM_BYTES = 64 * 1024 * 1024
SSD_CHUNK = 128
NORM_ROWS = 64
LOG2E = math.log2(math.e)


def _pick(n, target, mult):
    best = None
    for d in range(mult, min(n, target) + 1, mult):
        if n % d == 0:
            best = d
    assert best is not None, (n, target, mult)
    return best


def _round_up(x, m):
    return (x + m - 1) // m * m


def _cparams(sem, vmem_bytes):
    limit = int(min(max(vmem_bytes * 5 // 4 + (4 << 20), 16 << 20), V7X_VMEM_BYTES - (6 << 20)))
    return pltpu.CompilerParams(dimension_semantics=sem, vmem_limit_bytes=limit)


def _silu(x):
    return x * jax.nn.sigmoid(x)


class _Layout:
    def __init__(self, d_model):
        self.d = d_model
        self.w = d_model // 2
        self.h = self.w // SSD_HEAD_DIM
        self.gn = SSD_GROUPS * SSD_STATE
        self.conv = self.w + 2 * self.gn
        self.na = d_model // 4
        self.na_heads = self.na // HEAD_DIM
        self.ga = d_model // 4
        self.ga_heads = self.ga // HEAD_DIM
        self.kv_heads = self.ga_heads // GQA_REP
        self.kvw = self.kv_heads * HEAD_DIM
        self.dtw = _round_up(2 * self.h, LANES)
        xb = self.w + self.gn
        src = {}
        o = 0
        for name, width in (("xb", xb), ("dt", 2 * self.h), ("na_k", self.na), ("na_v", self.na),
                            ("ga_k", self.kvw), ("ga_v", self.kvw), ("ssd_c", self.gn), ("z", self.w),
                            ("na_q", self.na), ("na_gate", self.na), ("ga_q", self.ga), ("ga_gate", self.ga)):
            src[name] = (o, width)
            o += width
        self.src = src
        self.in_cols = o
        segs = (("xbc", self.conv, self.conv), ("na_q", self.na, self.na), ("z", self.w, self.w),
                ("na_k", self.na, self.na), ("na_v", self.na, self.na), ("na_gate", self.na, self.na),
                ("ga_q", self.ga, self.ga), ("ga_gate", self.ga, self.ga),
                ("ga_k", self.kvw, self.kvw), ("ga_v", self.kvw, self.kvw), ("dt", self.dtw, self.dtw))
        off = {}
        o = 0
        for name, width, align in segs:
            o = _round_up(o, align)
            off[name] = o
            o += width
        self.off = off
        self.tn = 512 if o > 512 else LANES
        self.np_ = _round_up(o, self.tn)


def _deinterleave_heads(w, heads):
    lead = w.shape[:-1]
    w = w.reshape(lead + (heads, HEAD_DIM // 2, 2))
    w = jnp.swapaxes(w, -1, -2)
    return w.reshape(lead + (heads * HEAD_DIM,))


def _prep_w_in(w_in_l, lay):
    d = w_in_l.shape[0]

    def col(name):
        o, wd = lay.src[name]
        return w_in_l[:, o:o + wd]

    pieces = {
        "xbc": jnp.concatenate([col("xb"), col("ssd_c")], axis=1),
        "na_q": col("na_q"), "z": col("z"), "na_k": col("na_k"), "na_v": col("na_v"),
        "na_gate": col("na_gate"),
        "ga_q": _deinterleave_heads(col("ga_q"), lay.ga_heads),
        "ga_gate": col("ga_gate"),
        "ga_k": _deinterleave_heads(col("ga_k"), lay.kv_heads),
        "ga_v": col("ga_v"),
        "dt": col("dt"),
    }
    out = []
    pos = 0
    for name, o in sorted(lay.off.items(), key=lambda kv: kv[1]):
        if o > pos:
            out.append(jnp.zeros((d, o - pos), w_in_l.dtype))
        out.append(pieces[name])
        pos = o + pieces[name].shape[1]
    if lay.np_ > pos:
        out.append(jnp.zeros((d, lay.np_ - pos), w_in_l.dtype))
    return jnp.concatenate(out, axis=1).astype(BF16)


def _mod_kernel(c_ref, w_ref, b_ref, o_ref):
    sc = _silu(c_ref[...]).astype(BF16)
    o_ref[0] = jnp.dot(sc, w_ref[0].astype(BF16), preferred_element_type=F32) + b_ref[0]


def _modulation(cc, w_mod, b_mod):
    nl, d, n3 = w_mod.shape
    tn = _pick(n3, 512, LANES)
    vmem = 2 * (d * tn * 4) + d * tn * 2 + 8 * d * 4
    return pl.pallas_call(
        _mod_kernel,
        grid=(nl, n3 // tn),
        in_specs=[pl.BlockSpec((SUBLANES, d), lambda l, j: (0, 0)),
                  pl.BlockSpec((1, d, tn), lambda l, j: (l, 0, j)),
                  pl.BlockSpec((1, 1, tn), lambda l, j: (l, 0, j))],
        out_specs=pl.BlockSpec((1, SUBLANES, tn), lambda l, j: (l, 0, j)),
        out_shape=jax.ShapeDtypeStruct((nl, SUBLANES, n3), F32),
        compiler_params=_cparams(("arbitrary", "arbitrary"), vmem),
        name="adaln_modulation",
    )(cc, w_mod, b_mod.reshape(nl, 1, n3))


def _inproj_kernel(h_ref, g_ref, ml_ref, mc_ref, w_ref, o_ref, xn_ref, *, s_len, tm):
    i = pl.program_id(0)
    j = pl.program_id(1)

    @pl.when(j == 0)
    def _():
        rc = NORM_ROWS

        def body(cidx, carry):
            r0 = pl.multiple_of(cidx * rc, rc)
            x = h_ref[pl.ds(r0, rc), :]
            y = x * lax.rsqrt(jnp.mean(x * x, axis=-1, keepdims=True) + EPS) * g_ref[...]
            row = i * tm + r0 + lax.broadcasted_iota(jnp.int32, (rc, 1), 0)
            is_lat = row < s_len
            shift = jnp.where(is_lat, ml_ref[0:1, :], mc_ref[0:1, :])
            scale = jnp.where(is_lat, ml_ref[1:2, :], mc_ref[1:2, :])
            xn_ref[pl.ds(r0, rc), :] = (y * (1.0 + scale) + shift).astype(BF16)
            return carry

        lax.fori_loop(0, tm // rc, body, 0)

    o_ref[...] = jnp.dot(xn_ref[...], w_ref[...], preferred_element_type=F32)


def _inproj(hh, g, mod_l, mod_c, wp, s_len, tn):
    t, d = hh.shape
    np_ = wp.shape[1]
    tm = _pick(t, 640, LANES)
    vmem = 2 * tm * d * 4 + tm * d * 2 + 2 * d * tn * 2 + 2 * tm * tn * 4 + 3 * tm * d * 4
    return pl.pallas_call(
        functools.partial(_inproj_kernel, s_len=s_len, tm=tm),
        grid=(t // tm, np_ // tn),
        in_specs=[pl.BlockSpec((tm, d), lambda i, j: (i, 0)),
                  pl.BlockSpec((1, d), lambda i, j: (0, 0)),
                  pl.BlockSpec((3, d), lambda i, j: (0, 0)),
                  pl.BlockSpec((3, d), lambda i, j: (0, 0)),
                  pl.BlockSpec((d, tn), lambda i, j: (0, j))],
        out_specs=pl.BlockSpec((tm, tn), lambda i, j: (i, j)),
        out_shape=jax.ShapeDtypeStruct((t, np_), F32),
        scratch_shapes=[pltpu.VMEM((tm, d), BF16)],
        compiler_params=_cparams(("arbitrary", "arbitrary"), vmem),
        name="norm_inproj",
    )(hh, g.reshape(1, d), mod_l, mod_c, wp)


def _conv_kernel(x_ref, prev_ref, next_ref, w_ref, b_ref, o_ref, *, s_len, t_len, tt):
    i = pl.program_id(0)
    x = x_ref[...]
    rid = lax.broadcasted_iota(jnp.int32, (tt, 1), 0)
    g = i * tt + rid
    xm1 = jnp.where(rid == 0, prev_ref[SUBLANES - 1:SUBLANES, :], pltpu.roll(x, 1, 0))
    xp1 = jnp.where(rid == tt - 1, next_ref[0:1, :], pltpu.roll(x, tt - 1, 0))
    xm1 = jnp.where((g == 0) | (g == s_len), 0.0, xm1)
    xp1 = jnp.where((g == s_len - 1) | (g == t_len - 1), 0.0, xp1)
    y = xm1 * w_ref[0:1, :] + x * w_ref[1:2, :] + xp1 * w_ref[2:3, :] + b_ref[...]
    o_ref[...] = _silu(y)


def _ssd_conv(p, conv_w, conv_b, s_len, lay):
    t = p.shape[0]
    c = lay.conv
    assert lay.off["xbc"] == 0
    tt = _pick(t, 256, LANES)
    nb8 = t // SUBLANES
    r8 = tt // SUBLANES
    vmem = 4 * tt * c * 4 + 6 * tt * c * 4
    return pl.pallas_call(
        functools.partial(_conv_kernel, s_len=s_len, t_len=t, tt=tt),
        grid=(t // tt,),
        in_specs=[pl.BlockSpec((tt, c), lambda i: (i, 0)),
                  pl.BlockSpec((SUBLANES, c), lambda i: (jnp.maximum(i * r8 - 1, 0), 0)),
                  pl.BlockSpec((SUBLANES, c), lambda i: (jnp.minimum((i + 1) * r8, nb8 - 1), 0)),
                  pl.BlockSpec((SSD_CONV, c), lambda i: (0, 0)),
                  pl.BlockSpec((1, c), lambda i: (0, 0))],
        out_specs=pl.BlockSpec((tt, c), lambda i: (i, 0)),
        out_shape=jax.ShapeDtypeStruct((t, c), F32),
        compiler_params=_cparams(("arbitrary",), vmem),
        name="ssd_conv_silu",
    )(p, p, p, conv_w, conv_b.reshape(1, c))


def _ssd_scan_kernel(u_ref, dt_ref, bias_ref, alog_ref, y_ref, state_ref, *, rev, heads, width, lc):
    k = pl.program_id(0)
    p_dim = SSD_HEAD_DIM
    n = SSD_STATE
    r = heads // SSD_GROUPS
    gn = SSD_GROUPS * n

    @pl.when(k == 0)
    def _():
        state_ref[...] = jnp.zeros_like(state_ref)

    nr = dt_ref.shape[1]
    dt = jax.nn.softplus(dt_ref[...].T + bias_ref[...])
    a = dt * (-jnp.exp(alog_ref[...]))
    lane = lax.broadcasted_iota(jnp.int32, (nr, lc), 1)
    ac = a
    sh = 1
    while sh < lc:
        if rev:
            ac = ac + jnp.where(lane < lc - sh, pltpu.roll(ac, lc - sh, 1), 0.0)
        else:
            ac = ac + jnp.where(lane >= sh, pltpu.roll(ac, sh, 1), 0.0)
        sh *= 2
    end = 0 if rev else lc - 1
    tot = ac[:, end:end + 1]
    decay_w = jnp.exp(tot - ac) * dt
    e_in = jnp.exp(ac)
    cdec = jnp.exp(jnp.broadcast_to(tot, (nr, n)))
    ac_n = ac.T
    row0 = heads if rev else 0

    x_t = u_ref[:, :width].T
    x_tb = x_t.astype(BF16)
    bm = u_ref[:, width:width + gn].astype(BF16)
    cm = u_ref[:, width + gn:width + 2 * gn].astype(BF16)
    li = lax.broadcasted_iota(jnp.int32, (lc, lc), 0)
    si = lax.broadcasted_iota(jnp.int32, (lc, lc), 1)
    tri = (li <= si) if rev else (li >= si)
    nt = (((1,), (1,)), ((), ()))

    y_parts = []
    for g in range(SSD_GROUPS):
        bg = bm[:, g * n:(g + 1) * n]
        cg = cm[:, g * n:(g + 1) * n]
        cb = lax.dot_general(cg, bg, nt, preferred_element_type=F32)
        prev = state_ref[g * r * p_dim:(g + 1) * r * p_dim, :]
        y_off = lax.dot_general(prev.astype(BF16), cg, nt, preferred_element_type=F32)
        xw_parts = []
        cd_parts = []
        for rr in range(r):
            h = g * r + rr
            hr = row0 + h
            seg = jnp.where(tri, jnp.exp(ac_n[:, hr:hr + 1] - ac[hr:hr + 1, :]), 0.0)
            gp = (cb * seg * dt[hr:hr + 1, :]).astype(BF16)
            xh = x_tb[h * p_dim:(h + 1) * p_dim, :]
            y_diag = lax.dot_general(xh, gp, nt, preferred_element_type=F32)
            y_parts.append(y_diag + y_off[rr * p_dim:(rr + 1) * p_dim, :] * e_in[hr:hr + 1, :])
            xw_parts.append((x_t[h * p_dim:(h + 1) * p_dim, :] * decay_w[hr:hr + 1, :]).astype(BF16))
            cd_parts.append(jnp.broadcast_to(cdec[hr:hr + 1, :], (p_dim, n)))
        xw = jnp.concatenate(xw_parts, axis=0)
        state_ref[g * r * p_dim:(g + 1) * r * p_dim, :] = (
            jnp.concatenate(cd_parts, axis=0) * prev + jnp.dot(xw, bg, preferred_element_type=F32))
    y_ref[...] = jnp.concatenate(y_parts, axis=0).T


def _ssd_scan(u, p, dt_bias_col, a_log_col, lay, s_len, rev):
    t = u.shape[0]
    lc = SSD_CHUNK
    nc = t // lc
    nl = s_len // lc
    assert t % lc == 0 and s_len % lc == 0
    dt_blk = lay.off["dt"] // lay.dtw
    if rev:
        cidx = lambda k: nc - 1 - k
    else:
        cidx = lambda k: (k + nl) % nc
    vmem = 2 * lc * lay.conv * 4 + 2 * lc * lay.w * 4 + lay.w * SSD_STATE * 4 + 12 * lc * lay.w * 4
    return pl.pallas_call(
        functools.partial(_ssd_scan_kernel, rev=rev, heads=lay.h, width=lay.w, lc=lc),
        grid=(nc,),
        in_specs=[pl.BlockSpec((lc, lay.conv), lambda k: (cidx(k), 0)),
                  pl.BlockSpec((lc, lay.dtw), lambda k: (cidx(k), dt_blk)),
                  pl.BlockSpec((lay.dtw, 1), lambda k: (0, 0)),
                  pl.BlockSpec((lay.dtw, 1), lambda k: (0, 0))],
        out_specs=pl.BlockSpec((lc, lay.w), lambda k: (cidx(k), 0)),
        out_shape=jax.ShapeDtypeStruct((t, lay.w), F32),
        scratch_shapes=[pltpu.VMEM((lay.w, SSD_STATE), F32)],
        compiler_params=_cparams(("arbitrary",), vmem),
        name="ssd_scan_bwd" if rev else "ssd_scan_fwd",
    )(u, p, dt_bias_col, a_log_col)


def _ssd_out_kernel(yf_ref, yb_ref, x_ref, z_ref, d_ref, g_ref, o_ref, *, width):
    y = yf_ref[...] + yb_ref[...] + x_ref[...] * d_ref[...]
    u = y * _silu(z_ref[...])
    gw = width // SSD_GROUPS
    parts = []
    for g in range(SSD_GROUPS):
        ug = u[:, g * gw:(g + 1) * gw]
        parts.append(ug * lax.rsqrt(jnp.mean(ug * ug, axis=-1, keepdims=True) + EPS))
    o_ref[...] = (jnp.concatenate(parts, axis=1) * g_ref[...]).astype(o_ref.dtype)


def _ssd_out(yf, yb, u, p, d_skip_row, norm_g, lay):
    t = yf.shape[0]
    w = lay.w
    tt = _pick(t, 256, LANES)
    zb = lay.off["z"] // w
    vmem = 2 * 4 * tt * w * 4 + 2 * tt * w * 2 + 4 * tt * w * 4
    return pl.pallas_call(
        functools.partial(_ssd_out_kernel, width=w),
        grid=(t // tt,),
        in_specs=[pl.BlockSpec((tt, w), lambda i: (i, 0)),
                  pl.BlockSpec((tt, w), lambda i: (i, 0)),
                  pl.BlockSpec((tt, w), lambda i: (i, 0)),
                  pl.BlockSpec((tt, w), lambda i: (i, zb)),
                  pl.BlockSpec((1, w), lambda i: (0, 0)),
                  pl.BlockSpec((1, w), lambda i: (0, 0))],
        out_specs=pl.BlockSpec((tt, w), lambda i: (i, 0)),
        out_shape=jax.ShapeDtypeStruct((t, w), BF16),
        compiler_params=_cparams(("arbitrary",), vmem),
        name="ssd_gate_norm",
    )(yf, yb, u, p, d_skip_row, norm_g.reshape(1, w))


def _na_bias_kernel(rpb_ref, o_ref, *, heads):
    e = pl.program_id(0)
    h = pl.program_id(1)
    nd_r = 2 * NA_WIN_ROWS - 1
    nd_c = 2 * NA_WIN_COLS - 1
    per_tile = LANES // GRID_W
    x = lax.broadcasted_iota(jnp.int32, (GRID_W, LANES), 0)
    lane = lax.broadcasted_iota(jnp.int32, (GRID_W, LANES), 1)
    y = lane % GRID_W
    jj = lane // GRID_W
    diff = y - x + (NA_WIN_COLS - 1)
    cs = jnp.clip(x - NA_WIN_COLS // 2, 0, GRID_W - NA_WIN_COLS)
    inwin = (y >= cs) & (y < cs + NA_WIN_COLS)
    for tile in range(NA_WIN_ROWS // per_tile):
        acc = jnp.zeros((GRID_W, LANES), F32)
        for d in range(nd_c):
            val = jnp.zeros((GRID_W, LANES), F32)
            for q in range(per_tile):
                j = tile * per_tile + q
                dr = j - e + (NA_WIN_ROWS - 1)
                v = rpb_ref[(h * nd_r + dr) * nd_c + d]
                val = jnp.where(jj == q, v, val)
            acc = jnp.where(diff == d, val, acc)
        o_ref[0, 0, :, tile * LANES:(tile + 1) * LANES] = jnp.where(inwin, acc, -jnp.inf)


def _na_bias(rpb_l, heads):
    nwin = NA_WIN_ROWS * GRID_W
    flat = rpb_l.reshape(-1)
    return pl.pallas_call(
        functools.partial(_na_bias_kernel, heads=heads),
        grid=(NA_WIN_ROWS, heads),
        in_specs=[pl.BlockSpec(memory_space=pltpu.SMEM)],
        out_specs=pl.BlockSpec((1, 1, GRID_W, nwin), lambda e, h: (e, h, 0, 0)),
        out_shape=jax.ShapeDtypeStruct((NA_WIN_ROWS, heads, GRID_W, nwin), F32),
        compiler_params=_cparams(("arbitrary", "arbitrary"), 1 << 20),
        name="na_bias_table",
    )(flat)


def _na_kernel(q_ref, kw_ref, vw_ref, kc_ref, vc_ref, gate_ref, bias_ref, _alias_ref, o_ref, *, heads):
    scale = HEAD_DIM ** -0.5
    nt = (((1,), (1,)), ((), ()))
    for h in range(heads):
        sl = slice(h * HEAD_DIM, (h + 1) * HEAD_DIM)
        q = q_ref[:, sl].astype(BF16)
        s_w = lax.dot_general(q, kw_ref[:, sl].astype(BF16), nt, preferred_element_type=F32) * scale
        s_w = s_w + bias_ref[0, h]
        s_c = lax.dot_general(q, kc_ref[:, sl].astype(BF16), nt, preferred_element_type=F32) * scale
        m = jnp.maximum(jnp.max(s_w, axis=1, keepdims=True), jnp.max(s_c, axis=1, keepdims=True))
        p_w = jnp.exp(s_w - m)
        p_c = jnp.exp(s_c - m)
        l = jnp.sum(p_w, axis=1, keepdims=True) + jnp.sum(p_c, axis=1, keepdims=True)
        o = (jnp.dot(p_w.astype(BF16), vw_ref[:, sl].astype(BF16), preferred_element_type=F32)
             + jnp.dot(p_c.astype(BF16), vc_ref[:, sl].astype(BF16), preferred_element_type=F32))
        o_ref[:, sl] = (o / l * _silu(gate_ref[:, sl])).astype(o_ref.dtype)


def _na_latent(p, bias, lay, s_len, t_len, out):
    rows = s_len // GRID_W
    wr = NA_WIN_ROWS
    assert rows >= wr
    na = lay.na
    nwin = wr * GRID_W
    tc = t_len - s_len
    qb = lay.off["na_q"] // na
    gb = lay.off["na_gate"] // na
    k_off = lay.off["na_k"]
    v_off = lay.off["na_v"]
    cblk = s_len // tc

    def rs(i):
        return jnp.clip(i - wr // 2, 0, rows - wr)

    vmem = 2 * (2 * GRID_W * na * 4 + 2 * nwin * na * 4 + 2 * tc * na * 4
                + lay.na_heads * GRID_W * nwin * 4 + GRID_W * na * 2) + 4 * nwin * na * 4
    return pl.pallas_call(
        functools.partial(_na_kernel, heads=lay.na_heads),
        grid=(rows,),
        in_specs=[pl.BlockSpec((GRID_W, na), lambda i: (i, qb)),
                  pl.BlockSpec((pl.Element(nwin), pl.Element(na)), lambda i: (rs(i) * GRID_W, k_off)),
                  pl.BlockSpec((pl.Element(nwin), pl.Element(na)), lambda i: (rs(i) * GRID_W, v_off)),
                  pl.BlockSpec((tc, na), lambda i: (cblk, k_off // na)),
                  pl.BlockSpec((tc, na), lambda i: (cblk, v_off // na)),
                  pl.BlockSpec((GRID_W, na), lambda i: (i, gb)),
                  pl.BlockSpec((1, lay.na_heads, GRID_W, nwin), lambda i: (i - rs(i), 0, 0, 0)),
                  pl.BlockSpec(memory_space=pl.ANY)],
        out_specs=pl.BlockSpec((GRID_W, na), lambda i: (i, 0)),
        out_shape=jax.ShapeDtypeStruct((t_len, na), BF16),
        input_output_aliases={7: 0},
        compiler_params=_cparams(("arbitrary",), vmem),
        name="na_latent",
    )(p, p, p, p, p, p, bias, out)


def _qk_prep_kernel(q_ref, k_ref, v_ref, cos_ref, sin_ref, qg_ref, kg_ref, qo_ref, ko_ref, vo_ref, *,
                    q_heads, kv_heads):
    c = cos_ref[...]
    s = sin_ref[...]

    def norm_rope(x, g):
        y = x * lax.rsqrt(jnp.mean(x * x, axis=-1, keepdims=True) + EPS) * g
        return y * c + pltpu.roll(y, HEAD_DIM // 2, 1) * s

    for h in range(q_heads):
        sl = slice(h * HEAD_DIM, (h + 1) * HEAD_DIM)
        qo_ref[:, sl] = norm_rope(q_ref[:, sl], qg_ref[...]).astype(BF16)
    for h in range(kv_heads):
        sl = slice(h * HEAD_DIM, (h + 1) * HEAD_DIM)
        ko_ref[:, sl] = norm_rope(k_ref[:, sl], kg_ref[...]).astype(BF16)
    vo_ref[...] = v_ref[...].astype(BF16)


def _qk_prep(p, cos_t, sin_t, qg, kg, lay):
    t = p.shape[0]
    tt = _pick(t, 256, LANES)
    ga, kvw = lay.ga, lay.kvw
    vmem = 2 * (tt * ga * 4 + 2 * tt * kvw * 4 + 2 * tt * LANES * 4 + tt * ga * 2 + 2 * tt * kvw * 2) + 4 * tt * ga * 4
    return pl.pallas_call(
        functools.partial(_qk_prep_kernel, q_heads=lay.ga_heads, kv_heads=lay.kv_heads),
        grid=(t // tt,),
        in_specs=[pl.BlockSpec((tt, ga), lambda i: (i, lay.off["ga_q"] // ga)),
                  pl.BlockSpec((tt, kvw), lambda i: (i, lay.off["ga_k"] // kvw)),
                  pl.BlockSpec((tt, kvw), lambda i: (i, lay.off["ga_v"] // kvw)),
                  pl.BlockSpec((tt, HEAD_DIM), lambda i: (i, 0)),
                  pl.BlockSpec((tt, HEAD_DIM), lambda i: (i, 0)),
                  pl.BlockSpec((1, HEAD_DIM), lambda i: (0, 0)),
                  pl.BlockSpec((1, HEAD_DIM), lambda i: (0, 0))],
        out_specs=[pl.BlockSpec((tt, ga), lambda i: (i, 0)),
                   pl.BlockSpec((tt, kvw), lambda i: (i, 0)),
                   pl.BlockSpec((tt, kvw), lambda i: (i, 0))],
        out_shape=[jax.ShapeDtypeStruct((t, ga), BF16),
                   jax.ShapeDtypeStruct((t, kvw), BF16),
                   jax.ShapeDtypeStruct((t, kvw), BF16)],
        compiler_params=_cparams(("arbitrary",), vmem),
        name="gqa_qk_norm_rope",
    )(p, p, p, cos_t, sin_t, qg, kg)


def _flash_kernel(q_ref, k_ref, v_ref, gate_ref, _alias_ref, o_ref, m_ref, l_ref, acc_ref, *, rep, nk):
    j = pl.program_id(2)
    qscale = (HEAD_DIM ** -0.5) * LOG2E
    nt = (((1,), (1,)), ((), ()))

    @pl.when(j == 0)
    def _():
        m_ref[...] = jnp.full_like(m_ref, -jnp.inf)
        l_ref[...] = jnp.zeros_like(l_ref)
        acc_ref[...] = jnp.zeros_like(acc_ref)

    k = k_ref[...].astype(BF16)
    v = v_ref[...].astype(BF16)
    tk = k.shape[0]
    for h in range(rep):
        sl = slice(h * HEAD_DIM, (h + 1) * HEAD_DIM)
        q = (q_ref[:, sl].astype(F32) * qscale).astype(BF16)
        s = lax.dot_general(q, k, nt, preferred_element_type=F32)
        m_prev = m_ref[h]
        m_new = jnp.maximum(m_prev, jnp.max(s, axis=1, keepdims=True))
        alpha = jnp.exp2(m_prev - m_new)
        p = jnp.exp2(s - jnp.tile(m_new, (1, tk // LANES)))
        l_ref[h] = alpha * l_ref[h] + jnp.sum(p, axis=1, keepdims=True)
        acc_ref[h] = alpha * acc_ref[h] + jnp.dot(p.astype(BF16), v, preferred_element_type=F32)
        m_ref[h] = m_new

    @pl.when(j == nk - 1)
    def _():
        for h in range(rep):
            sl = slice(h * HEAD_DIM, (h + 1) * HEAD_DIM)
            o = acc_ref[h] / l_ref[h]
            o_ref[:, sl] = (o * _silu(gate_ref[:, sl].astype(F32))).astype(o_ref.dtype)


def _attention(q, k, v, gate, out, *, q_col, k_col, v_col, gate_col, o_col, q_heads, rep,
               q_row0, nq_rows, k_row0, nk_rows, tq, tk, name):
    t_out, ow = out.shape
    gw = rep * HEAD_DIM
    groups = q_heads // rep
    assert nq_rows % tq == 0 and nk_rows % tk == 0 and q_row0 % tq == 0 and k_row0 % tk == 0
    assert tk % LANES == 0
    nq, nk = nq_rows // tq, nk_rows // tk
    qi0, kj0 = q_row0 // tq, k_row0 // tk
    vmem = (2 * (tq * gw * (q.dtype.itemsize + gate.dtype.itemsize + 2) + 2 * tk * HEAD_DIM * k.dtype.itemsize)
            + 3 * rep * tq * LANES * 4 + 6 * tq * tk * 4)
    return pl.pallas_call(
        functools.partial(_flash_kernel, rep=rep, nk=nk),
        grid=(groups, nq, nk),
        in_specs=[pl.BlockSpec((tq, gw), lambda g, i, j: (qi0 + i, q_col // gw + g)),
                  pl.BlockSpec((tk, HEAD_DIM), lambda g, i, j: (kj0 + j, k_col // HEAD_DIM + g)),
                  pl.BlockSpec((tk, HEAD_DIM), lambda g, i, j: (kj0 + j, v_col // HEAD_DIM + g)),
                  pl.BlockSpec((tq, gw), lambda g, i, j: (qi0 + i, gate_col // gw + g)),
                  pl.BlockSpec(memory_space=pl.ANY)],
        out_specs=pl.BlockSpec((tq, gw), lambda g, i, j: (qi0 + i, o_col // gw + g)),
        out_shape=jax.ShapeDtypeStruct((t_out, ow), out.dtype),
        input_output_aliases={4: 0},
        scratch_shapes=[pltpu.VMEM((rep, tq, LANES), F32),
                        pltpu.VMEM((rep, tq, LANES), F32),
                        pltpu.VMEM((rep, tq, HEAD_DIM), F32)],
        compiler_params=_cparams(("arbitrary", "arbitrary", "arbitrary"), vmem),
        name=name,
    )(q, k, v, gate, out)


def _outproj_kernel(a1_ref, a2_ref, a3_ref, w1_ref, w2_ref, w3_ref, h_ref, gl_ref, gc_ref, o_ref, *,
                    s_len, tm):
    i = pl.program_id(0)
    acc = jnp.dot(a1_ref[...], w1_ref[...], preferred_element_type=F32)
    acc = acc + jnp.dot(a2_ref[...], w2_ref[...], preferred_element_type=F32)
    acc = acc + jnp.dot(a3_ref[...], w3_ref[...], preferred_element_type=F32)
    row = i * tm + lax.broadcasted_iota(jnp.int32, (tm, 1), 0)
    gate = jnp.where(row < s_len, gl_ref[...], gc_ref[...])
    o_ref[...] = h_ref[...] + gate * acc


def _outproj(a1, a2, a3, wo, hh, gate_l, gate_c, s_len):
    t, d = hh.shape
    w1, w2, w3 = a1.shape[1], a2.shape[1], a3.shape[1]
    assert w1 % w2 == 0 and w2 == w3
    tm = _pick(t, 640, LANES)
    tn = _pick(d, 1024, LANES)
    vmem = 2 * (tm * (w1 + w2 + w3) * 2 + (w1 + w2 + w3) * tn * 2 + 2 * tm * tn * 4) + 2 * tm * tn * 4
    return pl.pallas_call(
        functools.partial(_outproj_kernel, s_len=s_len, tm=tm),
        grid=(t // tm, d // tn),
        in_specs=[pl.BlockSpec((tm, w1), lambda i, j: (i, 0)),
                  pl.BlockSpec((tm, w2), lambda i, j: (i, 0)),
                  pl.BlockSpec((tm, w3), lambda i, j: (i, 0)),
                  pl.BlockSpec((w1, tn), lambda i, j: (0, j)),
                  pl.BlockSpec((w2, tn), lambda i, j: (w1 // w2, j)),
                  pl.BlockSpec((w3, tn), lambda i, j: (w1 // w2 + 1, j)),
                  pl.BlockSpec((tm, tn), lambda i, j: (i, j)),
                  pl.BlockSpec((1, tn), lambda i, j: (0, j)),
                  pl.BlockSpec((1, tn), lambda i, j: (0, j))],
        out_specs=pl.BlockSpec((tm, tn), lambda i, j: (i, j)),
        out_shape=jax.ShapeDtypeStruct((t, d), F32),
        compiler_params=_cparams(("arbitrary", "arbitrary"), vmem),
        name="outproj_residual",
    )(a1, a2, a3, wo, wo, wo, hh, gate_l, gate_c)


def _final_norm_kernel(h_ref, g_ref, o_ref):
    x = h_ref[...]
    o_ref[...] = x * lax.rsqrt(jnp.mean(x * x, axis=-1, keepdims=True) + EPS) * g_ref[...]


def _final_norm(hh, g, s_len):
    d = hh.shape[1]
    tt = _pick(s_len, 256, SUBLANES)
    return pl.pallas_call(
        _final_norm_kernel,
        grid=(s_len // tt,),
        in_specs=[pl.BlockSpec((tt, d), lambda i: (i, 0)),
                  pl.BlockSpec((1, d), lambda i: (0, 0))],
        out_specs=pl.BlockSpec((tt, d), lambda i: (i, 0)),
        out_shape=jax.ShapeDtypeStruct((s_len, d), F32),
        compiler_params=_cparams(("arbitrary",), 6 * tt * d * 4),
        name="final_rmsnorm",
    )(hh, g.reshape(1, d))


def _rope_tables(s_len, t_len):
    tpos = jnp.arange(s_len)
    row = (tpos // GRID_W).astype(F32)
    colp = (tpos % GRID_W).astype(F32)
    nfreq = HEAD_DIM // 4
    freqs = ROPE_THETA ** (-jnp.arange(nfreq, dtype=F32) / nfreq)
    ang = jnp.concatenate([row[:, None] * freqs, colp[:, None] * freqs], axis=-1)
    cos, sin = jnp.cos(ang), jnp.sin(ang)
    cos_t = jnp.concatenate([cos, cos], axis=-1)
    sin_t = jnp.concatenate([-sin, sin], axis=-1)
    pad = t_len - s_len
    cos_t = jnp.concatenate([cos_t, jnp.ones((pad, HEAD_DIM), F32)], axis=0)
    sin_t = jnp.concatenate([sin_t, jnp.zeros((pad, HEAD_DIM), F32)], axis=0)
    return cos_t, sin_t


def _deinterleave_vec(g):
    return jnp.concatenate([g[0::2], g[1::2]]).reshape(1, HEAD_DIM)


def _forward_one(x, c, ctx, c_ctx, w_mod, b_mod, norm_g, w_in, conv_w, conv_b, a_log, dt_bias, d_skip,
                 ssd_norm_g, na_rpb, q_norm_g, k_norm_g, w_out, final_norm_g):
    s_len, d = x.shape
    tc = ctx.shape[0]
    t_len = s_len + tc
    depth = w_in.shape[0]
    lay = _Layout(d)
    assert w_in.shape[2] == lay.in_cols and s_len % tc == 0 and tc % SSD_CHUNK == 0

    cc = jnp.zeros((SUBLANES, d), F32).at[0].set(c).at[1].set(c_ctx)
    mods = _modulation(cc, w_mod, b_mod).reshape(depth, SUBLANES, 3, d)
    cos_t, sin_t = _rope_tables(s_len, t_len)
    hh = jnp.concatenate([x, ctx], axis=0)

    tq = _pick(s_len, 512, LANES)
    tk = _pick(t_len, 640, LANES)
    for l in range(depth):
        wp = _prep_w_in(w_in[l], lay)
        wo = w_out[l].astype(BF16)
        mod_l, mod_c = mods[l, 0], mods[l, 1]
        p = _inproj(hh, norm_g[l], mod_l, mod_c, wp, s_len, lay.tn)

        u = _ssd_conv(p, conv_w[l], conv_b[l], s_len, lay)
        pad = lay.dtw - 2 * lay.h
        bias_col = jnp.pad(dt_bias[l].reshape(-1), (0, pad)).reshape(lay.dtw, 1)
        alog_col = jnp.pad(a_log[l].reshape(-1), (0, pad)).reshape(lay.dtw, 1)
        yf = _ssd_scan(u, p, bias_col, alog_col, lay, s_len, rev=False)
        yb = _ssd_scan(u, p, bias_col, alog_col, lay, s_len, rev=True)
        d_row = jnp.repeat(d_skip[l], SSD_HEAD_DIM).reshape(1, lay.w)
        ssd_o = _ssd_out(yf, yb, u, p, d_row, ssd_norm_g[l], lay)

        bias = _na_bias(na_rpb[l], lay.na_heads)
        na_o = jnp.zeros((t_len, lay.na), BF16)
        na_o = _na_latent(p, bias, lay, s_len, t_len, na_o)
        na_o = _attention(p, p, p, p, na_o, q_col=lay.off["na_q"], k_col=lay.off["na_k"],
                          v_col=lay.off["na_v"], gate_col=lay.off["na_gate"], o_col=0,
                          q_heads=lay.na_heads, rep=1, q_row0=s_len, nq_rows=tc, k_row0=s_len, nk_rows=tc,
                          tq=tc, tk=tc, name="na_context")

        qn, kn, vb = _qk_prep(p, cos_t, sin_t, _deinterleave_vec(q_norm_g[l]), _deinterleave_vec(k_norm_g[l]), lay)
        ga_o = jnp.zeros((t_len, lay.ga), BF16)
        ga_o = _attention(qn, kn, vb, p, ga_o, q_col=0, k_col=0, v_col=0, gate_col=lay.off["ga_gate"], o_col=0,
                          q_heads=lay.ga_heads, rep=GQA_REP, q_row0=0, nq_rows=s_len, k_row0=0, nk_rows=t_len,
                          tq=tq, tk=tk, name="gqa_latent")
        ga_o = _attention(qn, kn, vb, p, ga_o, q_col=0, k_col=0, v_col=0, gate_col=lay.off["ga_gate"], o_col=0,
                          q_heads=lay.ga_heads, rep=GQA_REP, q_row0=s_len, nq_rows=tc, k_row0=s_len, nk_rows=tc,
                          tq=tc, tk=tc, name="gqa_context")

        hh = _outproj(ssd_o, na_o, ga_o, wo, hh, mod_l[2:3], mod_c[2:3], s_len)

    return _final_norm(hh, final_norm_g, s_len)


def kernel(x, c, ctx, c_ctx, w_mod, b_mod, norm_g, w_in, conv_w, conv_b, a_log, dt_bias, d_skip, ssd_norm_g,
           na_rpb, q_norm_g, k_norm_g, w_out, final_norm_g):
    outs = [
        _forward_one(x[b], c[b], ctx[b], c_ctx, w_mod, b_mod, norm_g, w_in, conv_w, conv_b, a_log, dt_bias,
                     d_skip, ssd_norm_g, na_rpb, q_norm_g, k_norm_g, w_out, final_norm_g)
        for b in range(x.shape[0])
    ]
    return jnp.stack(outs, axis=0)
```
